```python
import math
import jax, jax.numpy as jnp
from jax import lax
import numpy as np

D_MODEL = 1024
BATCH = 8
SEQ = 2048
DEPTH = 2
DEC_BATCH = 128
DEC_SEQ = 1
PAST_LEN = 16384
PAGE_SIZE = 128

N_MIXERS = 2
N_SSD_LAYERS = (DEPTH + N_MIXERS - 1) // N_MIXERS
N_SC_LAYERS = DEPTH // N_MIXERS
SSD_EXPAND = 2
D_INNER = SSD_EXPAND * D_MODEL
SSD_HEAD_DIM = 64
SSD_HEADS = D_INNER // SSD_HEAD_DIM
SSD_GROUPS = 4
D_STATE = 128
SSD_CONV_K = 4
SSD_CONV_DIM = D_INNER + 2 * SSD_GROUPS * D_STATE
SSD_IN_DIM = D_INNER + SSD_CONV_DIM + SSD_HEADS
CHUNK = 128
SC_EXPAND = 2
D_SC = SC_EXPAND * D_MODEL
SC_CONV_K = 3
SC_IN_DIM = 4 * D_SC
EPS = 1e-5

kernel_name = "ssd_shortconv_hybrid_step"


def rmsnorm(x, w):
    xf = x.astype(jnp.float32)
    out = xf * lax.rsqrt(jnp.mean(xf * xf, axis=-1, keepdims=True) + EPS)
    return (out * w.astype(jnp.float32)).astype(x.dtype)


def grouped_rmsnorm(y, w, groups):
    b, L, d = y.shape
    yf = y.astype(jnp.float32).reshape(b, L, groups, d // groups)
    yf = yf * lax.rsqrt(jnp.mean(yf * yf, axis=-1, keepdims=True) + EPS)
    return (yf.reshape(b, L, d) * w.astype(jnp.float32)).astype(y.dtype)


def causal_dwconv(u, buf, w):
    K = w.shape[0]
    L = u.shape[1]
    ext = jnp.concatenate([buf.astype(u.dtype), u], axis=1)
    out = ext[:, 0:L] * w[0]
    for k in range(1, K):
        out = out + ext[:, k:k + L] * w[k]
    return out, ext[:, L:]


def ssd_chunked(x, dt, A, Bm, Cm, h0):
    b, L, H, P = x.shape
    G, N = Bm.shape[2], Bm.shape[3]
    E = H // G
    Q = math.gcd(L, CHUNK)
    nc = L // Q
    f32 = jnp.float32
    a_cum = jnp.cumsum((dt * A).reshape(b, nc, Q, H), axis=2)
    xd = (x.astype(f32) * dt[..., None]).reshape(b, nc, Q, G, E, P)
    Bc = Bm.astype(f32).reshape(b, nc, Q, G, N)
    Cc = Cm.astype(f32).reshape(b, nc, Q, G, N)
    causal = jnp.tril(jnp.ones((Q, Q), dtype=bool))[None, None, :, :, None]
    seg = a_cum[:, :, :, None, :] - a_cum[:, :, None, :, :]
    Lmat = jnp.exp(jnp.where(causal, seg, -jnp.inf)).reshape(b, nc, Q, Q, G, E)
    CB = jnp.einsum("bclgn,bcsgn->bclsg", Cc, Bc)
    y_diag = jnp.einsum("bclsge,bcsgep->bclgep", CB[..., None] * Lmat, xd)
    decay_to_end = jnp.exp(a_cum[:, :, -1:, :] - a_cum).reshape(b, nc, Q, G, E)
    chunk_states = jnp.einsum("bcsgn,bcsge,bcsgep->bcgepn", Bc, decay_to_end, xd).reshape(b, nc, H, P, N)
    chunk_decay = jnp.exp(a_cum[:, :, -1, :])

    def step(h, inp):
        s, d = inp
        return h * d[:, :, None, None] + s, h

    h_final, h_prev = lax.scan(step, h0.astype(f32),
                               (jnp.moveaxis(chunk_states, 1, 0), jnp.moveaxis(chunk_decay, 1, 0)))
    h_prev = jnp.moveaxis(h_prev, 0, 1).reshape(b, nc, G, E, P, N)
    y_off = jnp.einsum("bclgn,bcgepn,bclge->bclgep", Cc, h_prev,
                       jnp.exp(a_cum).reshape(b, nc, Q, G, E))
    y = (y_diag + y_off).reshape(b, L, H, P)
    return y.astype(x.dtype), h_final.astype(h0.dtype)


def ssd_mixer(h, conv_buf, ssm_state, w_in, conv_w, conv_b, dt_bias, A_log, Dskip, norm_w, w_out):
    b, L, _ = h.shape
    proj = h @ w_in
    z = proj[..., :D_INNER]
    xBC = proj[..., D_INNER:D_INNER + SSD_CONV_DIM]
    dt = proj[..., D_INNER + SSD_CONV_DIM:]
    xBC, new_buf = causal_dwconv(xBC, conv_buf, conv_w)
    xBC = jax.nn.silu(xBC + conv_b)
    xs = xBC[..., :D_INNER].reshape(b, L, SSD_HEADS, SSD_HEAD_DIM)
    Bm = xBC[..., D_INNER:D_INNER + SSD_GROUPS * D_STATE].reshape(b, L, SSD_GROUPS, D_STATE)
    Cm = xBC[..., D_INNER + SSD_GROUPS * D_STATE:].reshape(b, L, SSD_GROUPS, D_STATE)
    dt = jax.nn.softplus((dt + dt_bias).astype(jnp.float32))
    A = -jnp.exp(A_log.astype(jnp.float32))
    y, new_state = ssd_chunked(xs, dt, A, Bm, Cm, ssm_state)
    y = y + Dskip[:, None] * xs
    y = y.reshape(b, L, D_INNER) * jax.nn.silu(z)
    y = grouped_rmsnorm(y, norm_w, SSD_GROUPS)
    return y @ w_out, new_buf, new_state


def short_conv_mixer(h, buf, w_in, conv_w, w_out):
    proj = h @ w_in
    bg = proj[..., :D_SC]
    cg = proj[..., D_SC:2 * D_SC]
    xin = proj[..., 2 * D_SC:3 * D_SC]
    z = proj[..., 3 * D_SC:]
    conv, new_buf = causal_dwconv(cg * xin, buf, conv_w)
    y = bg * conv * jax.nn.silu(z)
    return y @ w_out, new_buf


def setup_inputs(seed: int = 0) -> dict:
    key = jax.random.key(seed)
    ks = jax.random.split(key, 20)
    f32 = jnp.float32
    nrm = lambda k, s, sc: jax.random.normal(k, s, f32) * sc
    dt_init = jnp.exp(jax.random.uniform(ks[9], (N_SSD_LAYERS, SSD_HEADS), f32,
                                         minval=math.log(1e-3), maxval=math.log(1e-1)))
    return {
        "x_prompt": nrm(ks[0], (BATCH, SEQ, D_MODEL), 1.0),
        "x_sample": nrm(ks[1], (DEC_BATCH, DEC_SEQ, D_MODEL), 1.0),
        "state_ssm": nrm(ks[2], (N_SSD_LAYERS, DEC_BATCH, SSD_HEADS, SSD_HEAD_DIM, D_STATE), 0.3),
        "state_ssm_conv": nrm(ks[3], (N_SSD_LAYERS, DEC_BATCH, SSD_CONV_K - 1, SSD_CONV_DIM), 1.0),
        "state_sconv": nrm(ks[4], (N_SC_LAYERS, DEC_BATCH, SC_CONV_K - 1, D_SC), 1.0),
        "norm_w": 1.0 + nrm(ks[5], (DEPTH, D_MODEL), 0.02),
        "final_norm_w": 1.0 + nrm(ks[6], (D_MODEL,), 0.02),
        "ssd_w_in": nrm(ks[7], (N_SSD_LAYERS, D_MODEL, SSD_IN_DIM), D_MODEL ** -0.5),
        "ssd_conv_w": nrm(ks[8], (N_SSD_LAYERS, SSD_CONV_K, SSD_CONV_DIM), SSD_CONV_K ** -0.5),
        "ssd_conv_b": nrm(ks[10], (N_SSD_LAYERS, SSD_CONV_DIM), 0.02),
        "ssd_dt_bias": dt_init + jnp.log(-jnp.expm1(-dt_init)),
        "ssd_A_log": jnp.log(jax.random.uniform(ks[11], (N_SSD_LAYERS, SSD_HEADS), f32, minval=1.0, maxval=16.0)),
        "ssd_D": 1.0 + nrm(ks[12], (N_SSD_LAYERS, SSD_HEADS), 0.02),
        "ssd_norm_w": 1.0 + nrm(ks[13], (N_SSD_LAYERS, D_INNER), 0.02),
        "ssd_w_out": nrm(ks[14], (N_SSD_LAYERS, D_INNER, D_MODEL), D_INNER ** -0.5),
        "sc_w_in": nrm(ks[15], (N_SC_LAYERS, D_MODEL, SC_IN_DIM), D_MODEL ** -0.5),
        "sc_conv_w": nrm(ks[16], (N_SC_LAYERS, SC_CONV_K, D_SC), SC_CONV_K ** -0.5),
        "sc_w_out": nrm(ks[17], (N_SC_LAYERS, D_SC, D_MODEL), D_SC ** -0.5),
    }


def reference(x_prompt, x_sample, state_ssm, state_ssm_conv, state_sconv, norm_w, final_norm_w,
              ssd_w_in, ssd_conv_w, ssd_conv_b, ssd_dt_bias, ssd_A_log, ssd_D, ssd_norm_w, ssd_w_out,
              sc_w_in, sc_conv_w, sc_w_out):
    bp = x_prompt.shape[0]
    dt_ = x_prompt.dtype
    yp, ys = x_prompt, x_sample
    ssm_p, conv_p, sc_p = [], [], []
    ssm_s, conv_s, sc_s = [], [], []
    for i in range(DEPTH):
        j = i // N_MIXERS
        if i % N_MIXERS == 0:
            params = (ssd_w_in[j], ssd_conv_w[j], ssd_conv_b[j], ssd_dt_bias[j], ssd_A_log[j],
                      ssd_D[j], ssd_norm_w[j], ssd_w_out[j])
            zero_buf = jnp.zeros((bp, SSD_CONV_K - 1, SSD_CONV_DIM), dt_)
            zero_h = jnp.zeros((bp, SSD_HEADS, SSD_HEAD_DIM, D_STATE), dt_)
            out_p, cb_p, h_p = ssd_mixer(rmsnorm(yp, norm_w[i]), zero_buf, zero_h, *params)
            out_s, cb_s, h_s = ssd_mixer(rmsnorm(ys, norm_w[i]), state_ssm_conv[j], state_ssm[j], *params)
            yp = yp + out_p
            ys = ys + out_s
            ssm_p.append(h_p); conv_p.append(cb_p)
            ssm_s.append(h_s); conv_s.append(cb_s)
        else:
            zero_buf = jnp.zeros((bp, SC_CONV_K - 1, D_SC), dt_)
            out_p, b_p = short_conv_mixer(rmsnorm(yp, norm_w[i]), zero_buf, sc_w_in[j], sc_conv_w[j], sc_w_out[j])
            out_s, b_s = short_conv_mixer(rmsnorm(ys, norm_w[i]), state_sconv[j], sc_w_in[j], sc_conv_w[j], sc_w_out[j])
            yp = yp + out_p
            ys = ys + out_s
            sc_p.append(b_p)
            sc_s.append(b_s)
    y_prompt = rmsnorm(yp, final_norm_w)
    y_sample = rmsnorm(ys, final_norm_w)
    new_ssm_p = jnp.stack(ssm_p)
    new_ssm_conv_p = jnp.stack(conv_p)
    new_sconv_p = jnp.stack(sc_p)
    new_ssm_s = jnp.stack(ssm_s)
    new_ssm_conv_s = jnp.stack(conv_s)
    new_sconv_s = jnp.stack(sc_s)
    return (y_prompt, y_sample, new_ssm_p, new_ssm_conv_p, new_sconv_p, new_ssm_s, new_ssm_conv_s, new_sconv_s)
```

```python
import functools

import jax
import jax.numpy as jnp
from jax import lax
from jax.experimental import pallas as pl
from jax.experimental.pallas import tpu as pltpu

F32 = jnp.float32
BF16 = jnp.bfloat16

EPS = 1e-5
D_MODEL = 1024
D_INNER = 2048
HEADS = 32
HEAD_DIM = 64
GROUPS = 4
HEADS_PER_GROUP = HEADS // GROUPS
GROUP_WIDTH = D_INNER // GROUPS
D_STATE = 128
SSD_CONV_K = 4
CONV_DIM = D_INNER + 2 * GROUPS * D_STATE
D_SC = 2048
SC_CONV_K = 3
CHUNK = 128
LANES = 128
SUBLANES = 8

SEQ_TILE = 256
SAMPLE_BLOCK = 4
VMEM_LIMIT = 52 * 1024 * 1024


def _rms(x, w):
    ms = jnp.mean(x * x, axis=-1, keepdims=True)
    return x * lax.rsqrt(ms + EPS) * w


def _silu(x):
    return x / (1.0 + jnp.exp(-x))


def _softplus(x):
    return jnp.maximum(x, 0.0) + jnp.log1p(jnp.exp(-jnp.abs(x)))


def _cumsum_rows(a):
    row = lax.broadcasted_iota(jnp.int32, a.shape, 0)
    k = 1
    while k < a.shape[0]:
        a = a + jnp.where(row >= k, pltpu.roll(a, k, axis=0), 0.0)
        k *= 2
    return a


def _group_norm(y, w):
    parts = []
    for g in range(GROUPS):
        sl = slice(g * GROUP_WIDTH, (g + 1) * GROUP_WIDTH)
        yg = y[:, sl]
        ms = jnp.mean(yg * yg, axis=-1, keepdims=True)
        parts.append(yg * lax.rsqrt(ms + EPS) * w[:, sl])
    return jnp.concatenate(parts, axis=1)


def _dot(a, b):
    return jnp.dot(a, b, preferred_element_type=F32)


def _dot_nt(a, b):
    return lax.dot_general(a, b, (((1,), (1,)), ((), ())), preferred_element_type=F32)


def _prompt_ssd_kernel(x_ref, nw_ref, wz_ref, wxbc_ref, wdt_ref, cw_ref, cb_ref, dtb_ref, alog_ref,
                       dskip_ref, gnw_ref, wout_ref,
                       y_ref, ssm_ref, cbuf_ref,
                       hn_scr, xbc_scr, xc_scr, yg_scr, *, seq_tile):
    T = seq_tile
    pad = SUBLANES
    tail = SSD_CONV_K - 1

    @pl.when(pl.program_id(1) == 0)
    def _start_sequence():
        ssm_ref[...] = jnp.zeros_like(ssm_ref)
        xbc_scr[0:pad, :] = jnp.zeros((pad, CONV_DIM), F32)

    hn = _rms(x_ref[0], nw_ref[...]).astype(BF16)
    hn_scr[...] = hn

    xbc_scr[pad:pad + T, :] = _dot(hn, wxbc_ref[...])
    cw = cw_ref[...]
    conv = xbc_scr[pad - 3:pad - 3 + T, :] * cw[0:1]
    conv = conv + xbc_scr[pad - 2:pad - 2 + T, :] * cw[1:2]
    conv = conv + xbc_scr[pad - 1:pad - 1 + T, :] * cw[2:3]
    conv = conv + xbc_scr[pad:pad + T, :] * cw[3:4]
    xc_scr[...] = _silu(conv + cb_ref[...])
    last_rows = xbc_scr[pad + T - tail:pad + T, :]
    cbuf_ref[0] = last_rows
    xbc_scr[pad - tail:pad, :] = last_rows

    dt_all = _softplus(_dot(hn, wdt_ref[...]) + dtb_ref[...])
    a_coef = -jnp.exp(alog_ref[...])

    row_i = lax.broadcasted_iota(jnp.int32, (CHUNK, CHUNK), 0)
    col_i = lax.broadcasted_iota(jnp.int32, (CHUNK, CHUNK), 1)
    causal = row_i >= col_i
    low_half = col_i < HEAD_DIM

    for c in range(T // CHUNK):
        r0 = c * CHUNK
        dtc = dt_all[r0:r0 + CHUNK]
        a_cum = _cumsum_rows(dtc * a_coef)
        a_cum_t = a_cum.T
        dt_t = dtc.T
        a_last = a_cum_t[:, CHUNK - 1:CHUNK]
        w_t = dt_t * jnp.exp(a_last - a_cum_t)
        chunk_decay = jnp.broadcast_to(jnp.exp(a_last), (LANES, LANES))

        for g in range(GROUPS):
            b_g = xc_scr[r0:r0 + CHUNK, D_INNER + g * D_STATE:D_INNER + (g + 1) * D_STATE].astype(BF16)
            c_off = D_INNER + GROUPS * D_STATE
            c_g = xc_scr[r0:r0 + CHUNK, c_off + g * D_STATE:c_off + (g + 1) * D_STATE].astype(BF16)
            cb = _dot_nt(c_g, b_g)
            xs_g = xc_scr[r0:r0 + CHUNK, g * GROUP_WIDTH:(g + 1) * GROUP_WIDTH]

            xs_t = xs_g.T
            scaled, decay_rows = [], []
            for e in range(HEADS_PER_GROUP):
                h = g * HEADS_PER_GROUP + e
                scaled.append(xs_t[e * HEAD_DIM:(e + 1) * HEAD_DIM] * w_t[h:h + 1, :])
                decay_rows.append(jnp.broadcast_to(chunk_decay[h:h + 1, :], (HEAD_DIM, D_STATE)))
            xw_t = jnp.concatenate(scaled, axis=0).astype(BF16)
            rows = slice(g * GROUP_WIDTH, (g + 1) * GROUP_WIDTH)
            h_prev = ssm_ref[0, rows, :]
            y_off = _dot_nt(c_g, h_prev.astype(BF16))
            ssm_ref[0, rows, :] = h_prev * jnp.concatenate(decay_rows, axis=0) + _dot(xw_t, b_g)

            for k in range(HEADS_PER_GROUP // 2):
                xpair = xs_g[:, k * LANES:(k + 1) * LANES]
                x_blk = jnp.concatenate(
                    [jnp.where(low_half, xpair, 0.0), jnp.where(low_half, 0.0, xpair)], axis=0).astype(BF16)
                ms, es = [], []
                for h in (g * HEADS_PER_GROUP + 2 * k, g * HEADS_PER_GROUP + 2 * k + 1):
                    a_col = jnp.broadcast_to(a_cum[:, h:h + 1], (CHUNK, CHUNK))
                    lmat = jnp.where(causal, jnp.exp(a_col - a_cum_t[h:h + 1, :]), 0.0)
                    ms.append((cb * lmat * dt_t[h:h + 1, :]).astype(BF16))
                    es.append(jnp.exp(a_col))
                y_pair = _dot(jnp.concatenate(ms, axis=1), x_blk)
                y_pair = y_pair + y_off[:, k * LANES:(k + 1) * LANES] * jnp.where(low_half, es[0], es[1])
                col = g * GROUP_WIDTH + k * LANES
                yg_scr[r0:r0 + CHUNK, col:col + LANES] = y_pair + dskip_ref[:, col:col + LANES] * xpair

    z = _dot(hn_scr[...], wz_ref[...])
    yn = _group_norm(yg_scr[...] * _silu(z), gnw_ref[...]).astype(BF16)
    y_ref[0] = _dot(yn, wout_ref[...]) + x_ref[0]


def _const_spec(shape):
    zeros = (0,) * len(shape)
    return pl.BlockSpec(shape, lambda *_: zeros, pipeline_mode=pl.Buffered(1))


def _prompt_ssd(x, nw, wz, wxbc, wdt, cw, cb, dtb, alog, dskip, gnw, wout):
    B, L, _ = x.shape
    T = SEQ_TILE
    kern = functools.partial(_prompt_ssd_kernel, seq_tile=T)
    consts = (nw, wz, wxbc, wdt, cw, cb, dtb, alog, dskip, gnw, wout)
    return pl.pallas_call(
        kern,
        grid=(B, L // T),
        in_specs=[pl.BlockSpec((1, T, D_MODEL), lambda b, j: (b, j, 0))] + [_const_spec(a.shape) for a in consts],
        out_specs=[
            pl.BlockSpec((1, T, D_MODEL), lambda b, j: (b, j, 0)),
            pl.BlockSpec((1, D_INNER, D_STATE), lambda b, j: (b, 0, 0)),
            pl.BlockSpec((1, SSD_CONV_K - 1, CONV_DIM), lambda b, j: (b, 0, 0)),
        ],
        out_shape=[
            jax.ShapeDtypeStruct((B, L, D_MODEL), F32),
            jax.ShapeDtypeStruct((B, D_INNER, D_STATE), F32),
            jax.ShapeDtypeStruct((B, SSD_CONV_K - 1, CONV_DIM), F32),
        ],
        scratch_shapes=[
            pltpu.VMEM((T, D_MODEL), BF16),
            pltpu.VMEM((T + SUBLANES, CONV_DIM), F32),
            pltpu.VMEM((T, CONV_DIM), F32),
            pltpu.VMEM((T, D_INNER), F32),
        ],
        compiler_params=pltpu.CompilerParams(
            dimension_semantics=("arbitrary", "arbitrary"), vmem_limit_bytes=VMEM_LIMIT),
        name="prompt_ssd_layer",
    )(x, *consts)


def _prompt_sconv_kernel(x_ref, nw_ref, win_ref, cw_ref, wout_ref, fnw_ref,
                         y_ref, sbuf_ref, u_scr, *, seq_tile):
    T = seq_tile
    pad = SUBLANES
    tail = SC_CONV_K - 1

    @pl.when(pl.program_id(1) == 0)
    def _start_sequence():
        u_scr[0:pad, :] = jnp.zeros((pad, D_SC), F32)

    x = x_ref[0]
    hn = _rms(x, nw_ref[...]).astype(BF16)
    cg = _dot(hn, win_ref[:, D_SC:2 * D_SC])
    xin = _dot(hn, win_ref[:, 2 * D_SC:3 * D_SC])
    u_scr[pad:pad + T, :] = cg * xin
    cw = cw_ref[...]
    conv = u_scr[pad - 2:pad - 2 + T, :] * cw[0:1]
    conv = conv + u_scr[pad - 1:pad - 1 + T, :] * cw[1:2]
    conv = conv + u_scr[pad:pad + T, :] * cw[2:3]
    last_rows = u_scr[pad + T - tail:pad + T, :]
    sbuf_ref[0] = last_rows
    u_scr[pad - tail:pad, :] = last_rows

    bg = _dot(hn, win_ref[:, 0:D_SC])
    z = _dot(hn, win_ref[:, 3 * D_SC:4 * D_SC])
    yv = (bg * conv * _silu(z)).astype(BF16)
    out = _dot(yv, wout_ref[...]) + x
    y_ref[0] = _rms(out, fnw_ref[...])


def _prompt_sconv(x, nw, win, cw, wout, fnw):
    B, L, _ = x.shape
    T = SEQ_TILE
    kern = functools.partial(_prompt_sconv_kernel, seq_tile=T)
    consts = (nw, win, cw, wout, fnw)
    return pl.pallas_call(
        kern,
        grid=(B, L // T),
        in_specs=[pl.BlockSpec((1, T, D_MODEL), lambda b, j: (b, j, 0))] + [_const_spec(a.shape) for a in consts],
        out_specs=[
            pl.BlockSpec((1, T, D_MODEL), lambda b, j: (b, j, 0)),
            pl.BlockSpec((1, SC_CONV_K - 1, D_SC), lambda b, j: (b, 0, 0)),
        ],
        out_shape=[
            jax.ShapeDtypeStruct((B, L, D_MODEL), F32),
            jax.ShapeDtypeStruct((B, SC_CONV_K - 1, D_SC), F32),
        ],
        scratch_shapes=[pltpu.VMEM((T + SUBLANES, D_SC), F32)],
        compiler_params=pltpu.CompilerParams(
            dimension_semantics=("arbitrary", "arbitrary"), vmem_limit_bytes=VMEM_LIMIT),
        name="prompt_sconv_layer",
    )(x, *consts)


def _expand_heads(v):
    S = v.shape[0]
    low_half = lax.broadcasted_iota(jnp.int32, (S, LANES), 1) < HEAD_DIM
    parts = []
    for k in range(HEADS // 2):
        lo = jnp.broadcast_to(v[:, 2 * k:2 * k + 1], (S, LANES))
        hi = jnp.broadcast_to(v[:, 2 * k + 1:2 * k + 2], (S, LANES))
        parts.append(jnp.where(low_half, lo, hi))
    return jnp.concatenate(parts, axis=1)


def _sample_in_kernel(x_ref, nw_ref, wz_ref, wxbc_ref, wdt_ref, b0_ref, b1_ref, b2_ref, cw_ref, cb_ref,
                      dtb_ref, alog_ref, dskip_ref,
                      z_ref, xbc_ref, bm_ref, cm_ref, xdt_ref, decay_ref, decayx_ref, ypart_ref):
    hn = _rms(x_ref[...], nw_ref[...]).astype(BF16)
    z_ref[...] = _dot(hn, wz_ref[...])
    xbc = _dot(hn, wxbc_ref[...])
    xbc_ref[...] = xbc
    cw = cw_ref[...]
    conv = b0_ref[...] * cw[0:1]
    conv = conv + b1_ref[...] * cw[1:2]
    conv = conv + b2_ref[...] * cw[2:3]
    conv = conv + xbc * cw[3:4]
    xc = _silu(conv + cb_ref[...])
    xs = xc[:, :D_INNER]
    bm = xc[:, D_INNER:D_INNER + GROUPS * D_STATE]
    cm = xc[:, D_INNER + GROUPS * D_STATE:]
    bm_ref[...] = bm
    cm_ref[...] = cm

    dt = _softplus(_dot(hn, wdt_ref[...]) + dtb_ref[...])
    decay = jnp.exp(dt * -jnp.exp(alog_ref[...]))
    decay_ref[...] = decay
    decayx_ref[...] = _expand_heads(decay)
    xd = xs * _expand_heads(dt)
    xdt_ref[...] = xd.T

    cbx = []
    for g in range(GROUPS):
        sl = slice(g * D_STATE, (g + 1) * D_STATE)
        cbg = jnp.sum(cm[:, sl] * bm[:, sl], axis=-1, keepdims=True)
        cbx.append(jnp.broadcast_to(cbg, (xs.shape[0], GROUP_WIDTH)))
    ypart_ref[...] = jnp.concatenate(cbx, axis=1) * xd + dskip_ref[...] * xs


def _sample_in(x, nw, wz, wxbc, wdt, b0, b1, b2, cw, cb, dtb, alog, dskip):
    S = x.shape[0]
    shapes = [(S, D_INNER), (S, CONV_DIM), (S, GROUPS * D_STATE), (S, GROUPS * D_STATE), (D_INNER, S),
              (S, LANES), (S, D_INNER), (S, D_INNER)]
    return pl.pallas_call(
        _sample_in_kernel,
        out_shape=[jax.ShapeDtypeStruct(s, F32) for s in shapes],
        compiler_params=pltpu.CompilerParams(vmem_limit_bytes=VMEM_LIMIT),
        name="sample_ssd_in",
    )(x, nw, wz, wxbc, wdt, b0, b1, b2, cw, cb, dtb, alog, dskip)


def _sample_state_kernel(decay_ref, h0_ref, xdt_ref, bm_ref, cm_ref, hout_ref, yt_ref, *, block):
    S = xdt_ref.shape[1]

    @pl.when(pl.program_id(0) == 0)
    def _init():
        yt_ref[...] = jnp.zeros_like(yt_ref)

    lane = lax.broadcasted_iota(jnp.int32, (1, S), 1)
    for i in range(block):
        b = pl.program_id(0) * block + i
        sel = lane == b
        xd_col = jnp.sum(jnp.where(sel, xdt_ref[...], 0.0), axis=-1, keepdims=True)
        for g in range(GROUPS):
            rows = slice(g * GROUP_WIDTH, (g + 1) * GROUP_WIDTH)
            b_row = bm_ref[i, :, g * D_STATE:(g + 1) * D_STATE]
            c_row = cm_ref[i, :, g * D_STATE:(g + 1) * D_STATE]
            h0 = h0_ref[i, rows, :]
            y_col = jnp.sum(h0 * c_row, axis=-1, keepdims=True)
            yt_ref[rows, :] = jnp.where(sel, y_col, yt_ref[rows, :])
            upd = xd_col[rows] * b_row
            for e in range(HEADS_PER_GROUP):
                r = slice(g * GROUP_WIDTH + e * HEAD_DIM, g * GROUP_WIDTH + (e + 1) * HEAD_DIM)
                d = decay_ref[b, g * HEADS_PER_GROUP + e]
                hout_ref[i, r, :] = h0[e * HEAD_DIM:(e + 1) * HEAD_DIM] * d + upd[e * HEAD_DIM:(e + 1) * HEAD_DIM]


def _sample_state(decay, h0, xdt, bm, cm):
    S = h0.shape[0]
    blk = SAMPLE_BLOCK
    kern = functools.partial(_sample_state_kernel, block=blk)
    bm = bm.reshape(S, 1, GROUPS * D_STATE)
    cm = cm.reshape(S, 1, GROUPS * D_STATE)
    row_spec = pl.BlockSpec((blk, 1, GROUPS * D_STATE), lambda i: (i, 0, 0))
    return pl.pallas_call(
        kern,
        grid=(S // blk,),
        in_specs=[
            pl.BlockSpec(memory_space=pltpu.SMEM),
            pl.BlockSpec((blk, D_INNER, D_STATE), lambda i: (i, 0, 0)),
            pl.BlockSpec(xdt.shape, lambda i: (0, 0)),
            row_spec, row_spec,
        ],
        out_specs=[
            pl.BlockSpec((blk, D_INNER, D_STATE), lambda i: (i, 0, 0)),
            pl.BlockSpec((D_INNER, S), lambda i: (0, 0)),
        ],
        out_shape=[
            jax.ShapeDtypeStruct((S, D_INNER, D_STATE), F32),
            jax.ShapeDtypeStruct((D_INNER, S), F32),
        ],
        compiler_params=pltpu.CompilerParams(
            dimension_semantics=("arbitrary",), vmem_limit_bytes=VMEM_LIMIT),
        name="sample_ssd_state",
    )(decay, h0, xdt, bm, cm)


def _sample_out_kernel(x_ref, yt_ref, ypart_ref, decayx_ref, z_ref, gnw_ref, wout0_ref, nw1_ref, win_ref,
                       s0_ref, s1_ref, scw_ref, wout1_ref, fnw_ref,
                       y_ref, u_ref, x1_scr):
    y = ypart_ref[...] + decayx_ref[...] * yt_ref[...].T
    yn = _group_norm(y * _silu(z_ref[...]), gnw_ref[...]).astype(BF16)
    x1_scr[...] = x_ref[...] + _dot(yn, wout0_ref[...])

    hn = _rms(x1_scr[...], nw1_ref[...]).astype(BF16)
    cg = _dot(hn, win_ref[:, D_SC:2 * D_SC])
    xin = _dot(hn, win_ref[:, 2 * D_SC:3 * D_SC])
    u = cg * xin
    u_ref[...] = u
    cw = scw_ref[...]
    conv = s0_ref[...] * cw[0:1]
    conv = conv + s1_ref[...] * cw[1:2]
    conv = conv + u * cw[2:3]
    bg = _dot(hn, win_ref[:, 0:D_SC])
    z = _dot(hn, win_ref[:, 3 * D_SC:4 * D_SC])
    yv = (bg * conv * _silu(z)).astype(BF16)
    y_ref[...] = _rms(x1_scr[...] + _dot(yv, wout1_ref[...]), fnw_ref[...])


def _sample_out(x, yt, ypart, decayx, z, gnw, wout0, nw1, win, s0, s1, scw, wout1, fnw):
    S = x.shape[0]
    return pl.pallas_call(
        _sample_out_kernel,
        out_shape=[jax.ShapeDtypeStruct((S, D_MODEL), F32), jax.ShapeDtypeStruct((S, D_SC), F32)],
        scratch_shapes=[pltpu.VMEM((S, D_MODEL), F32)],
        compiler_params=pltpu.CompilerParams(vmem_limit_bytes=VMEM_LIMIT),
        name="sample_out",
    )(x, yt, ypart, decayx, z, gnw, wout0, nw1, win, s0, s1, scw, wout1, fnw)


def _pad_lanes(v):
    return jnp.pad(v, (0, LANES - v.shape[0])).reshape(1, LANES)


def kernel(x_prompt, x_sample, state_ssm, state_ssm_conv, state_sconv, norm_w, final_norm_w, ssd_w_in,
           ssd_conv_w, ssd_conv_b, ssd_dt_bias, ssd_A_log, ssd_D, ssd_norm_w, ssd_w_out, sc_w_in, sc_conv_w,
           sc_w_out):
    B = x_prompt.shape[0]
    S = x_sample.shape[0]

    w_in0 = ssd_w_in[0]
    wz = w_in0[:, :D_INNER].astype(BF16)
    wxbc = w_in0[:, D_INNER:D_INNER + CONV_DIM].astype(BF16)
    wdt = jnp.pad(w_in0[:, D_INNER + CONV_DIM:], ((0, 0), (0, LANES - HEADS))).astype(BF16)
    wout0 = ssd_w_out[0].astype(BF16)
    win1 = sc_w_in[0].astype(BF16)
    wout1 = sc_w_out[0].astype(BF16)
    nw0 = norm_w[0].reshape(1, D_MODEL)
    nw1 = norm_w[1].reshape(1, D_MODEL)
    fnw = final_norm_w.reshape(1, D_MODEL)
    cw0 = ssd_conv_w[0]
    cb0 = ssd_conv_b[0].reshape(1, CONV_DIM)
    dtb = _pad_lanes(ssd_dt_bias[0])
    alog = _pad_lanes(ssd_A_log[0])
    dskip = jnp.repeat(ssd_D[0], HEAD_DIM).reshape(1, D_INNER)
    gnw = ssd_norm_w[0].reshape(1, D_INNER)
    scw = sc_conv_w[0]

    y1, ssm_p, conv_p = _prompt_ssd(x_prompt, nw0, wz, wxbc, wdt, cw0, cb0, dtb, alog, dskip, gnw, wout0)
    y_prompt, sconv_p = _prompt_sconv(y1, nw1, win1, scw, wout1, fnw)

    xs2 = x_sample.reshape(S, D_MODEL)
    cbuf = state_ssm_conv[0]
    z, xbc_new, bm, cm, xdt, decay, decayx, ypart = _sample_in(
        xs2, nw0, wz, wxbc, wdt, cbuf[:, 0], cbuf[:, 1], cbuf[:, 2], cw0, cb0, dtb, alog, dskip)
    ssm_s, yt = _sample_state(decay[:, :HEADS], state_ssm[0].reshape(S, D_INNER, D_STATE), xdt, bm, cm)
    sbuf = state_sconv[0]
    y_sample, u_new = _sample_out(xs2, yt, ypart, decayx, z, gnw, wout0, nw1, win1, sbuf[:, 0], sbuf[:, 1],
                                  scw, wout1, fnw)

    return (
        y_prompt,
        y_sample.reshape(S, 1, D_MODEL),
        ssm_p.reshape(1, B, HEADS, HEAD_DIM, D_STATE),
        conv_p.reshape(1, B, SSD_CONV_K - 1, CONV_DIM),
        sconv_p.reshape(1, B, SC_CONV_K - 1, D_SC),
        ssm_s.reshape(1, S, HEADS, HEAD_DIM, D_STATE),
        jnp.stack([cbuf[:, 1], cbuf[:, 2], xbc_new], axis=1)[None],
        jnp.stack([sbuf[:, 1], u_new], axis=1)[None],
    )
```

```python
import functools

import jax
import jax.numpy as jnp
from jax import lax
from jax.experimental import pallas as pl
from jax.experimental.pallas import tpu as pltpu

F32 = jnp.float32
BF16 = jnp.bfloat16

EPS = 1e-5
LOG2E = 1.4426950408889634
D_MODEL = 1024
D_INNER = 2048
HEADS = 32
HEAD_DIM = 64
GROUPS = 4
HEADS_PER_GROUP = HEADS // GROUPS
GROUP_WIDTH = D_INNER // GROUPS
D_STATE = 128
SSD_CONV_K = 4
CONV_DIM = D_INNER + 2 * GROUPS * D_STATE
D_SC = 2048
SC_CONV_K = 3
CHUNK = 128
LANES = 128
SUBLANES = 8

SEQ_TILE = 256
SAMPLE_BLOCK = 4
VMEM_LIMIT = 52 * 1024 * 1024


def _rms(x, w):
    ms = jnp.mean(x * x, axis=-1, keepdims=True)
    return x * lax.rsqrt(ms + EPS) * w


def _silu(x):
    hx = 0.5 * x
    return hx + hx * jnp.tanh(hx)


def _shift_rows(x, carry, k):
    rolled = pltpu.roll(x, k, axis=0)
    row = lax.broadcasted_iota(jnp.int32, carry.shape, 0)
    top = jnp.where(row < k, pltpu.roll(carry, k, axis=0), rolled[0:SUBLANES])
    return jnp.concatenate([top, rolled[SUBLANES:]], axis=0)


def _softplus(x):
    return jnp.maximum(x, 0.0) + jnp.log1p(jnp.exp(-jnp.abs(x)))


def _cumsum_rows(a):
    row = lax.broadcasted_iota(jnp.int32, a.shape, 0)
    k = 1
    while k < a.shape[0]:
        a = a + jnp.where(row >= k, pltpu.roll(a, k, axis=0), 0.0)
        k *= 2
    return a


def _group_norm(y, w):
    parts = []
    for g in range(GROUPS):
        sl = slice(g * GROUP_WIDTH, (g + 1) * GROUP_WIDTH)
        yg = y[:, sl]
        ms = jnp.mean(yg * yg, axis=-1, keepdims=True)
        parts.append(yg * lax.rsqrt(ms + EPS) * w[:, sl])
    return jnp.concatenate(parts, axis=1)


def _dot(a, b):
    return jnp.dot(a, b, preferred_element_type=F32)


def _dot_nt(a, b):
    return lax.dot_general(a, b, (((1,), (1,)), ((), ())), preferred_element_type=F32)


def _prompt_ssd_kernel(x_ref, nw_ref, wz_ref, wxbc_ref, wdt_ref, cw_ref, cb_ref, dtb_ref, alog_ref,
                       dskip_ref, gnw_ref, wout_ref,
                       y_ref, ssm_ref, cbuf_ref,
                       hn_scr, tail_scr, xc_scr, yg_scr, *, seq_tile):
    T = seq_tile

    @pl.when(pl.program_id(1) == 0)
    def _start_sequence():
        ssm_ref[...] = jnp.zeros_like(ssm_ref)
        tail_scr[...] = jnp.zeros_like(tail_scr)

    hn = _rms(x_ref[0], nw_ref[...]).astype(BF16)
    hn_scr[...] = hn

    xbc = _dot(hn, wxbc_ref[...])
    prev = tail_scr[...]
    cw = cw_ref[...]
    x_m1 = _shift_rows(xbc, prev, 1)
    a = x_m1 * cw[0:1] + xbc * cw[1:2]
    a_prev = pltpu.roll(prev, 1, axis=0) * cw[0:1] + prev * cw[1:2]
    conv = _shift_rows(a, a_prev, 2) + (x_m1 * cw[2:3] + xbc * cw[3:4])
    xc_scr[...] = _silu(conv + cb_ref[...])
    tail_scr[...] = xbc[T - SUBLANES:T]
    cbuf_ref[0] = tail_scr[SUBLANES - (SSD_CONV_K - 1):SUBLANES, :]

    dt_all = _softplus(_dot(hn, wdt_ref[...]) + dtb_ref[...])
    a_coef = -jnp.exp(alog_ref[...]) * LOG2E

    row_i = lax.broadcasted_iota(jnp.int32, (CHUNK, CHUNK), 0)
    col_i = lax.broadcasted_iota(jnp.int32, (CHUNK, CHUNK), 1)
    causal = row_i >= col_i
    low_half = col_i < HEAD_DIM

    for c in range(T // CHUNK):
        r0 = c * CHUNK
        dtc = dt_all[r0:r0 + CHUNK]
        a_cum = _cumsum_rows(dtc * a_coef)
        a_cum_t = a_cum.T
        dt_t = dtc.T
        a_last = a_cum_t[:, CHUNK - 1:CHUNK]
        w_t = dt_t * jnp.exp2(a_last - a_cum_t)
        chunk_decay = jnp.broadcast_to(jnp.exp2(a_last), (LANES, LANES))
        src_t = a_cum_t - jnp.log2(dt_t)

        for g in range(GROUPS):
            b_g = xc_scr[r0:r0 + CHUNK, D_INNER + g * D_STATE:D_INNER + (g + 1) * D_STATE].astype(BF16)
            c_off = D_INNER + GROUPS * D_STATE
            c_g = xc_scr[r0:r0 + CHUNK, c_off + g * D_STATE:c_off + (g + 1) * D_STATE].astype(BF16)
            cb = _dot_nt(c_g, b_g)
            xs_g = xc_scr[r0:r0 + CHUNK, g * GROUP_WIDTH:(g + 1) * GROUP_WIDTH]

            xs_t = xs_g.T
            scaled, decay_rows = [], []
            for e in range(HEADS_PER_GROUP):
                h = g * HEADS_PER_GROUP + e
                scaled.append(xs_t[e * HEAD_DIM:(e + 1) * HEAD_DIM] * w_t[h:h + 1, :])
                decay_rows.append(jnp.broadcast_to(chunk_decay[h:h + 1, :], (HEAD_DIM, D_STATE)))
            xw_t = jnp.concatenate(scaled, axis=0).astype(BF16)
            rows = slice(g * GROUP_WIDTH, (g + 1) * GROUP_WIDTH)
            h_prev = ssm_ref[0, rows, :]
            y_off = _dot_nt(c_g, h_prev.astype(BF16))
            ssm_ref[0, rows, :] = h_prev * jnp.concatenate(decay_rows, axis=0) + _dot(xw_t, b_g)

            for k in range(HEADS_PER_GROUP // 2):
                xpair = xs_g[:, k * LANES:(k + 1) * LANES]
                x_blk = jnp.concatenate(
                    [jnp.where(low_half, xpair, 0.0), jnp.where(low_half, 0.0, xpair)], axis=0).astype(BF16)
                ms, es = [], []
                for h in (g * HEADS_PER_GROUP + 2 * k, g * HEADS_PER_GROUP + 2 * k + 1):
                    a_col = jnp.broadcast_to(a_cum[:, h:h + 1], (CHUNK, CHUNK))
                    lmat_dt = jnp.where(causal, jnp.exp2(a_col - src_t[h:h + 1, :]), 0.0)
                    ms.append((cb * lmat_dt).astype(BF16))
                    es.append(jnp.exp2(a_col))
                y_pair = _dot(jnp.concatenate(ms, axis=1), x_blk)
                y_pair = y_pair + y_off[:, k * LANES:(k + 1) * LANES] * jnp.where(low_half, es[0], es[1])
                col = g * GROUP_WIDTH + k * LANES
                yg_scr[r0:r0 + CHUNK, col:col + LANES] = y_pair + dskip_ref[:, col:col + LANES] * xpair

    z = _dot(hn_scr[...], wz_ref[...])
    yn = _group_norm(yg_scr[...] * _silu(z), gnw_ref[...]).astype(BF16)
    y_ref[0] = _dot(yn, wout_ref[...]) + x_ref[0]


def _const_spec(shape):
    zeros = (0,) * len(shape)
    return pl.BlockSpec(shape, lambda *_: zeros, pipeline_mode=pl.Buffered(1))


def _prompt_ssd(x, nw, wz, wxbc, wdt, cw, cb, dtb, alog, dskip, gnw, wout):
    B, L, _ = x.shape
    T = SEQ_TILE
    kern = functools.partial(_prompt_ssd_kernel, seq_tile=T)
    consts = (nw, wz, wxbc, wdt, cw, cb, dtb, alog, dskip, gnw, wout)
    return pl.pallas_call(
        kern,
        grid=(B, L // T),
        in_specs=[pl.BlockSpec((1, T, D_MODEL), lambda b, j: (b, j, 0))] + [_const_spec(a.shape) for a in consts],
        out_specs=[
            pl.BlockSpec((1, T, D_MODEL), lambda b, j: (b, j, 0)),
            pl.BlockSpec((1, D_INNER, D_STATE), lambda b, j: (b, 0, 0)),
            pl.BlockSpec((1, SSD_CONV_K - 1, CONV_DIM), lambda b, j: (b, 0, 0)),
        ],
        out_shape=[
            jax.ShapeDtypeStruct((B, L, D_MODEL), F32),
            jax.ShapeDtypeStruct((B, D_INNER, D_STATE), F32),
            jax.ShapeDtypeStruct((B, SSD_CONV_K - 1, CONV_DIM), F32),
        ],
        scratch_shapes=[
            pltpu.VMEM((T, D_MODEL), BF16),
            pltpu.VMEM((SUBLANES, CONV_DIM), F32),
            pltpu.VMEM((T, CONV_DIM), F32),
            pltpu.VMEM((T, D_INNER), F32),
        ],
        compiler_params=pltpu.CompilerParams(
            dimension_semantics=("arbitrary", "arbitrary"), vmem_limit_bytes=VMEM_LIMIT),
        name="prompt_ssd_layer",
    )(x, *consts)


def _prompt_sconv_kernel(x_ref, nw_ref, win_ref, cw_ref, wout_ref, fnw_ref,
                         y_ref, sbuf_ref, tail_scr, *, seq_tile):
    T = seq_tile

    @pl.when(pl.program_id(1) == 0)
    def _start_sequence():
        tail_scr[...] = jnp.zeros_like(tail_scr)

    x = x_ref[0]
    hn = _rms(x, nw_ref[...]).astype(BF16)
    cg = _dot(hn, win_ref[:, D_SC:2 * D_SC])
    xin = _dot(hn, win_ref[:, 2 * D_SC:3 * D_SC])
    u = cg * xin
    prev = tail_scr[...]
    cw = cw_ref[...]
    a = _shift_rows(u, prev, 1) * cw[0:1] + u * cw[1:2]
    a_prev = pltpu.roll(prev, 1, axis=0) * cw[0:1] + prev * cw[1:2]
    conv = _shift_rows(a, a_prev, 1) + u * cw[2:3]
    tail_scr[...] = u[T - SUBLANES:T]
    sbuf_ref[0] = tail_scr[SUBLANES - (SC_CONV_K - 1):SUBLANES, :]

    bg = _dot(hn, win_ref[:, 0:D_SC])
    z = _dot(hn, win_ref[:, 3 * D_SC:4 * D_SC])
    yv = (bg * conv * _silu(z)).astype(BF16)
    out = _dot(yv, wout_ref[...]) + x
    y_ref[0] = _rms(out, fnw_ref[...])


def _prompt_sconv(x, nw, win, cw, wout, fnw):
    B, L, _ = x.shape
    T = SEQ_TILE
    kern = functools.partial(_prompt_sconv_kernel, seq_tile=T)
    consts = (nw, win, cw, wout, fnw)
    return pl.pallas_call(
        kern,
        grid=(B, L // T),
        in_specs=[pl.BlockSpec((1, T, D_MODEL), lambda b, j: (b, j, 0))] + [_const_spec(a.shape) for a in consts],
        out_specs=[
            pl.BlockSpec((1, T, D_MODEL), lambda b, j: (b, j, 0)),
            pl.BlockSpec((1, SC_CONV_K - 1, D_SC), lambda b, j: (b, 0, 0)),
        ],
        out_shape=[
            jax.ShapeDtypeStruct((B, L, D_MODEL), F32),
            jax.ShapeDtypeStruct((B, SC_CONV_K - 1, D_SC), F32),
        ],
        scratch_shapes=[pltpu.VMEM((SUBLANES, D_SC), F32)],
        compiler_params=pltpu.CompilerParams(
            dimension_semantics=("arbitrary", "arbitrary"), vmem_limit_bytes=VMEM_LIMIT),
        name="prompt_sconv_layer",
    )(x, *consts)


def _expand_heads(v):
    S = v.shape[0]
    low_half = lax.broadcasted_iota(jnp.int32, (S, LANES), 1) < HEAD_DIM
    parts = []
    for k in range(HEADS // 2):
        lo = jnp.broadcast_to(v[:, 2 * k:2 * k + 1], (S, LANES))
        hi = jnp.broadcast_to(v[:, 2 * k + 1:2 * k + 2], (S, LANES))
        parts.append(jnp.where(low_half, lo, hi))
    return jnp.concatenate(parts, axis=1)


def _sample_in_kernel(x_ref, nw_ref, wz_ref, wxbc_ref, wdt_ref, b0_ref, b1_ref, b2_ref, cw_ref, cb_ref,
                      dtb_ref, alog_ref, dskip_ref,
                      z_ref, xbc_ref, bm_ref, cm_ref, xdt_ref, decay_ref, decayx_ref, ypart_ref):
    hn = _rms(x_ref[...], nw_ref[...]).astype(BF16)
    z_ref[...] = _dot(hn, wz_ref[...])
    xbc = _dot(hn, wxbc_ref[...])
    xbc_ref[...] = xbc
    cw = cw_ref[...]
    conv = b0_ref[...] * cw[0:1]
    conv = conv + b1_ref[...] * cw[1:2]
    conv = conv + b2_ref[...] * cw[2:3]
    conv = conv + xbc * cw[3:4]
    xc = _silu(conv + cb_ref[...])
    xs = xc[:, :D_INNER]
    bm = xc[:, D_INNER:D_INNER + GROUPS * D_STATE]
    cm = xc[:, D_INNER + GROUPS * D_STATE:]
    bm_ref[...] = bm
    cm_ref[...] = cm

    dt = _softplus(_dot(hn, wdt_ref[...]) + dtb_ref[...])
    decay = jnp.exp(dt * -jnp.exp(alog_ref[...]))
    decay_ref[...] = decay
    decayx_ref[...] = _expand_heads(decay)
    xd = xs * _expand_heads(dt)
    xdt_ref[...] = xd.T

    cbx = []
    for g in range(GROUPS):
        sl = slice(g * D_STATE, (g + 1) * D_STATE)
        cbg = jnp.sum(cm[:, sl] * bm[:, sl], axis=-1, keepdims=True)
        cbx.append(jnp.broadcast_to(cbg, (xs.shape[0], GROUP_WIDTH)))
    ypart_ref[...] = jnp.concatenate(cbx, axis=1) * xd + dskip_ref[...] * xs


def _sample_in(x, nw, wz, wxbc, wdt, b0, b1, b2, cw, cb, dtb, alog, dskip):
    S = x.shape[0]
    shapes = [(S, D_INNER), (S, CONV_DIM), (S, GROUPS * D_STATE), (S, GROUPS * D_STATE), (D_INNER, S),
              (S, LANES), (S, D_INNER), (S, D_INNER)]
    return pl.pallas_call(
        _sample_in_kernel,
        out_shape=[jax.ShapeDtypeStruct(s, F32) for s in shapes],
        compiler_params=pltpu.CompilerParams(vmem_limit_bytes=VMEM_LIMIT),
        name="sample_ssd_in",
    )(x, nw, wz, wxbc, wdt, b0, b1, b2, cw, cb, dtb, alog, dskip)


def _sample_state_kernel(decay_ref, h0_ref, xdt_ref, bm_ref, cm_ref, hout_ref, yt_ref, *, block):
    S = xdt_ref.shape[1]

    @pl.when(pl.program_id(0) == 0)
    def _init():
        yt_ref[...] = jnp.zeros_like(yt_ref)

    lane = lax.broadcasted_iota(jnp.int32, (1, S), 1)
    for i in range(block):
        b = pl.program_id(0) * block + i
        sel = lane == b
        xd_col = jnp.sum(jnp.where(sel, xdt_ref[...], 0.0), axis=-1, keepdims=True)
        for g in range(GROUPS):
            rows = slice(g * GROUP_WIDTH, (g + 1) * GROUP_WIDTH)
            b_row = bm_ref[i, :, g * D_STATE:(g + 1) * D_STATE]
            c_row = cm_ref[i, :, g * D_STATE:(g + 1) * D_STATE]
            h0 = h0_ref[i, rows, :]
            y_col = jnp.sum(h0 * c_row, axis=-1, keepdims=True)
            yt_ref[rows, :] = jnp.where(sel, y_col, yt_ref[rows, :])
            upd = xd_col[rows] * b_row
            for e in range(HEADS_PER_GROUP):
                r = slice(g * GROUP_WIDTH + e * HEAD_DIM, g * GROUP_WIDTH + (e + 1) * HEAD_DIM)
                d = decay_ref[b, g * HEADS_PER_GROUP + e]
                hout_ref[i, r, :] = h0[e * HEAD_DIM:(e + 1) * HEAD_DIM] * d + upd[e * HEAD_DIM:(e + 1) * HEAD_DIM]


def _sample_state(decay, h0, xdt, bm, cm):
    S = h0.shape[0]
    blk = SAMPLE_BLOCK
    kern = functools.partial(_sample_state_kernel, block=blk)
    bm = bm.reshape(S, 1, GROUPS * D_STATE)
    cm = cm.reshape(S, 1, GROUPS * D_STATE)
    row_spec = pl.BlockSpec((blk, 1, GROUPS * D_STATE), lambda i: (i, 0, 0))
    return pl.pallas_call(
        kern,
        grid=(S // blk,),
        in_specs=[
            pl.BlockSpec(memory_space=pltpu.SMEM),
            pl.BlockSpec((blk, D_INNER, D_STATE), lambda i: (i, 0, 0)),
            pl.BlockSpec(xdt.shape, lambda i: (0, 0)),
            row_spec, row_spec,
        ],
        out_specs=[
            pl.BlockSpec((blk, D_INNER, D_STATE), lambda i: (i, 0, 0)),
            pl.BlockSpec((D_INNER, S), lambda i: (0, 0)),
        ],
        out_shape=[
            jax.ShapeDtypeStruct((S, D_INNER, D_STATE), F32),
            jax.ShapeDtypeStruct((D_INNER, S), F32),
        ],
        compiler_params=pltpu.CompilerParams(
            dimension_semantics=("arbitrary",), vmem_limit_bytes=VMEM_LIMIT),
        name="sample_ssd_state",
    )(decay, h0, xdt, bm, cm)


def _sample_out_kernel(x_ref, yt_ref, ypart_ref, decayx_ref, z_ref, gnw_ref, wout0_ref, nw1_ref, win_ref,
                       s0_ref, s1_ref, scw_ref, wout1_ref, fnw_ref,
                       y_ref, u_ref, x1_scr):
    y = ypart_ref[...] + decayx_ref[...] * yt_ref[...].T
    yn = _group_norm(y * _silu(z_ref[...]), gnw_ref[...]).astype(BF16)
    x1_scr[...] = x_ref[...] + _dot(yn, wout0_ref[...])

    hn = _rms(x1_scr[...], nw1_ref[...]).astype(BF16)
    cg = _dot(hn, win_ref[:, D_SC:2 * D_SC])
    xin = _dot(hn, win_ref[:, 2 * D_SC:3 * D_SC])
    u = cg * xin
    u_ref[...] = u
    cw = scw_ref[...]
    conv = s0_ref[...] * cw[0:1]
    conv = conv + s1_ref[...] * cw[1:2]
    conv = conv + u * cw[2:3]
    bg = _dot(hn, win_ref[:, 0:D_SC])
    z = _dot(hn, win_ref[:, 3 * D_SC:4 * D_SC])
    yv = (bg * conv * _silu(z)).astype(BF16)
    y_ref[...] = _rms(x1_scr[...] + _dot(yv, wout1_ref[...]), fnw_ref[...])


def _sample_out(x, yt, ypart, decayx, z, gnw, wout0, nw1, win, s0, s1, scw, wout1, fnw):
    S = x.shape[0]
    return pl.pallas_call(
        _sample_out_kernel,
        out_shape=[jax.ShapeDtypeStruct((S, D_MODEL), F32), jax.ShapeDtypeStruct((S, D_SC), F32)],
        scratch_shapes=[pltpu.VMEM((S, D_MODEL), F32)],
        compiler_params=pltpu.CompilerParams(vmem_limit_bytes=VMEM_LIMIT),
        name="sample_out",
    )(x, yt, ypart, decayx, z, gnw, wout0, nw1, win, s0, s1, scw, wout1, fnw)


def _pad_lanes(v):
    return jnp.pad(v, (0, LANES - v.shape[0])).reshape(1, LANES)


def kernel(x_prompt, x_sample, state_ssm, state_ssm_conv, state_sconv, norm_w, final_norm_w, ssd_w_in,
           ssd_conv_w, ssd_conv_b, ssd_dt_bias, ssd_A_log, ssd_D, ssd_norm_w, ssd_w_out, sc_w_in, sc_conv_w,
           sc_w_out):
    B = x_prompt.shape[0]
    S = x_sample.shape[0]

    w_in0 = ssd_w_in[0]
    wz = w_in0[:, :D_INNER].astype(BF16)
    wxbc = w_in0[:, D_INNER:D_INNER + CONV_DIM].astype(BF16)
    wdt = jnp.pad(w_in0[:, D_INNER + CONV_DIM:], ((0, 0), (0, LANES - HEADS))).astype(BF16)
    wout0 = ssd_w_out[0].astype(BF16)
    win1 = sc_w_in[0].astype(BF16)
    wout1 = sc_w_out[0].astype(BF16)
    nw0 = norm_w[0].reshape(1, D_MODEL)
    nw1 = norm_w[1].reshape(1, D_MODEL)
    fnw = final_norm_w.reshape(1, D_MODEL)
    cw0 = ssd_conv_w[0]
    cb0 = ssd_conv_b[0].reshape(1, CONV_DIM)
    dtb = _pad_lanes(ssd_dt_bias[0])
    alog = _pad_lanes(ssd_A_log[0])
    dskip = jnp.repeat(ssd_D[0], HEAD_DIM).reshape(1, D_INNER)
    gnw = ssd_norm_w[0].reshape(1, D_INNER)
    scw = sc_conv_w[0]

    y1, ssm_p, conv_p = _prompt_ssd(x_prompt, nw0, wz, wxbc, wdt, cw0, cb0, dtb, alog, dskip, gnw, wout0)
    y_prompt, sconv_p = _prompt_sconv(y1, nw1, win1, scw, wout1, fnw)

    xs2 = x_sample.reshape(S, D_MODEL)
    cbuf = state_ssm_conv[0]
    z, xbc_new, bm, cm, xdt, decay, decayx, ypart = _sample_in(
        xs2, nw0, wz, wxbc, wdt, cbuf[:, 0], cbuf[:, 1], cbuf[:, 2], cw0, cb0, dtb, alog, dskip)
    ssm_s, yt = _sample_state(decay[:, :HEADS], state_ssm[0].reshape(S, D_INNER, D_STATE), xdt, bm, cm)
    sbuf = state_sconv[0]
    y_sample, u_new = _sample_out(xs2, yt, ypart, decayx, z, gnw, wout0, nw1, win1, sbuf[:, 0], sbuf[:, 1],
                                  scw, wout1, fnw)

    return (
        y_prompt,
        y_sample.reshape(S, 1, D_MODEL),
        ssm_p.reshape(1, B, HEADS, HEAD_DIM, D_STATE),
        conv_p.reshape(1, B, SSD_CONV_K - 1, CONV_DIM),
        sconv_p.reshape(1, B, SC_CONV_K - 1, D_SC),
        ssm_s.reshape(1, S, HEADS, HEAD_DIM, D_STATE),
        jnp.stack([cbuf[:, 1], cbuf[:, 2], xbc_new], axis=1)[None],
        jnp.stack([sbuf[:, 1], u_new], axis=1)[None],
    )
```

```python
import functools

import jax
import jax.numpy as jnp
from jax import lax
from jax.experimental import pallas as pl
from jax.experimental.pallas import tpu as pltpu

F32 = jnp.float32
BF16 = jnp.bfloat16

EPS = 1e-5
LOG2E = 1.4426950408889634
D_MODEL = 1024
D_INNER = 2048
HEADS = 32
HEAD_DIM = 64
GROUPS = 4
HEADS_PER_GROUP = HEADS // GROUPS
GROUP_WIDTH = D_INNER // GROUPS
D_STATE = 128
SSD_CONV_K = 4
CONV_DIM = D_INNER + 2 * GROUPS * D_STATE
D_SC = 2048
SC_CONV_K = 3
CHUNK = 128
LANES = 128
SUBLANES = 8

SEQ_TILE = 256
MM_PIECE = 512
SAMPLE_BLOCK = 4
VMEM_LIMIT = 52 * 1024 * 1024


def _rms(x, w):
    ms = jnp.mean(x * x, axis=-1, keepdims=True)
    return x * lax.rsqrt(ms + EPS) * w


def _silu(x):
    hx = 0.5 * x
    return hx + hx * jnp.tanh(hx)


def _shift_rows(x, carry, k):
    rolled = pltpu.roll(x, k, axis=0)
    row = lax.broadcasted_iota(jnp.int32, carry.shape, 0)
    top = jnp.where(row < k, pltpu.roll(carry, k, axis=0), rolled[0:SUBLANES])
    return jnp.concatenate([top, rolled[SUBLANES:]], axis=0)


def _softplus(x):
    return jnp.maximum(x, 0.0) + jnp.log1p(jnp.exp(-jnp.abs(x)))


def _cumsum_rows(a):
    row = lax.broadcasted_iota(jnp.int32, a.shape, 0)
    k = 1
    while k < a.shape[0]:
        a = a + jnp.where(row >= k, pltpu.roll(a, k, axis=0), 0.0)
        k *= 2
    return a


def _group_norm(y, w):
    parts = []
    for g in range(GROUPS):
        sl = slice(g * GROUP_WIDTH, (g + 1) * GROUP_WIDTH)
        yg = y[:, sl]
        ms = jnp.mean(yg * yg, axis=-1, keepdims=True)
        parts.append(yg * lax.rsqrt(ms + EPS) * w[:, sl])
    return jnp.concatenate(parts, axis=1)


def _dot(a, b):
    return jnp.dot(a, b, preferred_element_type=F32)


def _dot_nt(a, b):
    return lax.dot_general(a, b, (((1,), (1,)), ((), ())), preferred_element_type=F32)


def _prompt_ssd_kernel(xa_ref, xr_ref, nw_ref, wz_ref, wxbc_ref, wdt_ref, cw_ref, cb_ref, dtb_ref, alog_ref,
                       dskip_ref, gnw_ref, wout_ref,
                       y_ref, ssm_ref, cbuf_ref,
                       hn_scr, xbc_scr, dt_scr, z_scr, yn_scr, tail_scr, state_scr, xc_scr, yg_scr, gate_scr,
                       *, seq_tile, tiles_per_seq, n_tiles):
    T = seq_tile
    t = pl.program_id(0)

    @pl.when(t == 0)
    def _init():
        for ref in (hn_scr, xbc_scr, dt_scr, z_scr, yn_scr, tail_scr, state_scr):
            ref[...] = jnp.zeros_like(ref)

    def out_proj_piece(n):
        cs = slice(n * MM_PIECE, (n + 1) * MM_PIECE)
        y_ref[0, :, cs] = _dot(yn_scr[...], wout_ref[:, cs]) + xr_ref[0, :, cs]

    def z_piece(n):
        cs = slice(n * MM_PIECE, (n + 1) * MM_PIECE)
        z_scr[:, cs] = _dot(hn_scr[...], wz_ref[:, cs])

    def xbc_piece(n):
        cs = slice(n * MM_PIECE, (n + 1) * MM_PIECE)
        xbc_scr[:, cs] = _dot(hn_scr[...], wxbc_ref[:, cs])

    def dt_piece():
        dt_new = _softplus(_dot(hn_scr[...], wdt_ref[...]) + dtb_ref[...])
        dt_scr[...] = jnp.where(jnp.logical_and(t >= 1, t <= n_tiles), dt_new, 0.0)

    early = [functools.partial(out_proj_piece, n) for n in range(D_MODEL // MM_PIECE)]
    early += [functools.partial(z_piece, n) for n in range(D_INNER // MM_PIECE)]
    late = [functools.partial(xbc_piece, n) for n in range(CONV_DIM // MM_PIECE)] + [dt_piece]

    first = jnp.logical_and(lax.rem(t - 2, tiles_per_seq) == 0, t <= n_tiles + 1)
    dt_all = dt_scr[...]
    gate_scr[...] = _silu(z_scr[...])

    for n in range(CONV_DIM // MM_PIECE):
        cs = slice(n * MM_PIECE, (n + 1) * MM_PIECE)
        xbc = xbc_scr[:, cs]
        prev = jnp.where(first, 0.0, tail_scr[:, cs])
        cw = cw_ref[:, cs]
        x_m1 = _shift_rows(xbc, prev, 1)
        a = x_m1 * cw[0:1] + xbc * cw[1:2]
        a_prev = pltpu.roll(prev, 1, axis=0) * cw[0:1] + prev * cw[1:2]
        conv = _shift_rows(a, a_prev, 2) + (x_m1 * cw[2:3] + xbc * cw[3:4])
        xc_scr[:, cs] = _silu(conv + cb_ref[:, cs])
        if early:
            early.pop(0)()
    tail_scr[...] = xbc_scr[T - SUBLANES:T, :]
    cbuf_ref[0] = xbc_scr[T - (SSD_CONV_K - 1):T, :]
    late = early + late

    a_coef = -jnp.exp(alog_ref[...]) * LOG2E

    row_i = lax.broadcasted_iota(jnp.int32, (CHUNK, CHUNK), 0)
    col_i = lax.broadcasted_iota(jnp.int32, (CHUNK, CHUNK), 1)
    causal = row_i >= col_i
    low_half = col_i < HEAD_DIM

    for c in range(T // CHUNK):
        r0 = c * CHUNK
        dtc = dt_all[r0:r0 + CHUNK]
        a_cum = _cumsum_rows(dtc * a_coef)
        a_cum_t = a_cum.T
        dt_t = dtc.T
        a_last = a_cum_t[:, CHUNK - 1:CHUNK]
        w_t = dt_t * jnp.exp2(a_last - a_cum_t)
        chunk_decay = jnp.broadcast_to(jnp.exp2(a_last), (LANES, LANES))
        src_t = a_cum_t - jnp.log2(dt_t)

        for g in range(GROUPS):
            b_g = xc_scr[r0:r0 + CHUNK, D_INNER + g * D_STATE:D_INNER + (g + 1) * D_STATE].astype(BF16)
            c_off = D_INNER + GROUPS * D_STATE
            c_g = xc_scr[r0:r0 + CHUNK, c_off + g * D_STATE:c_off + (g + 1) * D_STATE].astype(BF16)
            cb = _dot_nt(c_g, b_g)
            xs_g = xc_scr[r0:r0 + CHUNK, g * GROUP_WIDTH:(g + 1) * GROUP_WIDTH]

            xs_t = xs_g.T
            scaled, decay_rows = [], []
            for e in range(HEADS_PER_GROUP):
                h = g * HEADS_PER_GROUP + e
                scaled.append(xs_t[e * HEAD_DIM:(e + 1) * HEAD_DIM] * w_t[h:h + 1, :])
                decay_rows.append(jnp.broadcast_to(chunk_decay[h:h + 1, :], (HEAD_DIM, D_STATE)))
            xw_t = jnp.concatenate(scaled, axis=0).astype(BF16)
            rows = slice(g * GROUP_WIDTH, (g + 1) * GROUP_WIDTH)
            h_prev = state_scr[rows, :]
            if c == 0:
                h_prev = jnp.where(first, 0.0, h_prev)
            y_off = _dot_nt(c_g, h_prev.astype(BF16))
            state_scr[rows, :] = h_prev * jnp.concatenate(decay_rows, axis=0) + _dot(xw_t, b_g)

            for k in range(HEADS_PER_GROUP // 2):
                xpair = xs_g[:, k * LANES:(k + 1) * LANES]
                x_blk = jnp.concatenate(
                    [jnp.where(low_half, xpair, 0.0), jnp.where(low_half, 0.0, xpair)], axis=0).astype(BF16)
                ms, es = [], []
                for h in (g * HEADS_PER_GROUP + 2 * k, g * HEADS_PER_GROUP + 2 * k + 1):
                    a_col = jnp.broadcast_to(a_cum[:, h:h + 1], (CHUNK, CHUNK))
                    lmat_dt = jnp.where(causal, jnp.exp2(a_col - src_t[h:h + 1, :]), 0.0)
                    ms.append((cb * lmat_dt).astype(BF16))
                    es.append(jnp.exp2(a_col))
                y_pair = _dot(jnp.concatenate(ms, axis=1), x_blk)
                y_pair = y_pair + y_off[:, k * LANES:(k + 1) * LANES] * jnp.where(low_half, es[0], es[1])
                col = g * GROUP_WIDTH + k * LANES
                yg_scr[r0:r0 + CHUNK, col:col + LANES] = y_pair + dskip_ref[:, col:col + LANES] * xpair

            if late:
                late.pop(0)()

    while late:
        late.pop(0)()
    ssm_ref[0] = state_scr[...]
    yn_scr[...] = _group_norm(yg_scr[...] * gate_scr[...], gnw_ref[...]).astype(BF16)

    hn_scr[...] = _rms(xa_ref[0], nw_ref[...]).astype(BF16)


def _const_spec(shape):
    zeros = (0,) * len(shape)
    return pl.BlockSpec(shape, lambda *_: zeros, pipeline_mode=pl.Buffered(1))


def _prompt_ssd(x, nw, wz, wxbc, wdt, cw, cb, dtb, alog, dskip, gnw, wout):
    B, L, _ = x.shape
    T = SEQ_TILE
    nj = L // T
    nt = B * nj
    kern = functools.partial(_prompt_ssd_kernel, seq_tile=T, tiles_per_seq=nj, n_tiles=nt)
    consts = (nw, wz, wxbc, wdt, cw, cb, dtb, alog, dskip, gnw, wout)

    def tile_map(lag):
        def index_map(t):
            tile = jnp.clip(t - lag, 0, nt - 1)
            return (tile // nj, tile % nj, 0)
        return index_map

    def seq_map(t):
        return (jnp.clip(t - 2, 0, nt - 1) // nj, 0, 0)

    return pl.pallas_call(
        kern,
        grid=(nt + 3,),
        in_specs=[pl.BlockSpec((1, T, D_MODEL), tile_map(0)), pl.BlockSpec((1, T, D_MODEL), tile_map(3))]
        + [_const_spec(a.shape) for a in consts],
        out_specs=[
            pl.BlockSpec((1, T, D_MODEL), tile_map(3)),
            pl.BlockSpec((1, D_INNER, D_STATE), seq_map),
            pl.BlockSpec((1, SSD_CONV_K - 1, CONV_DIM), seq_map),
        ],
        out_shape=[
            jax.ShapeDtypeStruct((B, L, D_MODEL), F32),
            jax.ShapeDtypeStruct((B, D_INNER, D_STATE), F32),
            jax.ShapeDtypeStruct((B, SSD_CONV_K - 1, CONV_DIM), F32),
        ],
        scratch_shapes=[
            pltpu.VMEM((T, D_MODEL), BF16),
            pltpu.VMEM((T, CONV_DIM), F32),
            pltpu.VMEM((T, LANES), F32),
            pltpu.VMEM((T, D_INNER), F32),
            pltpu.VMEM((T, D_INNER), BF16),
            pltpu.VMEM((SUBLANES, CONV_DIM), F32),
            pltpu.VMEM((D_INNER, D_STATE), F32),
            pltpu.VMEM((T, CONV_DIM), F32),
            pltpu.VMEM((T, D_INNER), F32),
            pltpu.VMEM((T, D_INNER), F32),
        ],
        compiler_params=pltpu.CompilerParams(
            dimension_semantics=("arbitrary",), vmem_limit_bytes=VMEM_LIMIT),
        name="prompt_ssd_layer",
    )(x, x, *consts)


def _prompt_sconv_kernel(x_ref, nw_ref, win_ref, cw_ref, wout_ref, fnw_ref,
                         y_ref, sbuf_ref, tail_scr, *, seq_tile):
    T = seq_tile

    @pl.when(pl.program_id(1) == 0)
    def _start_sequence():
        tail_scr[...] = jnp.zeros_like(tail_scr)

    x = x_ref[0]
    hn = _rms(x, nw_ref[...]).astype(BF16)
    cg = _dot(hn, win_ref[:, D_SC:2 * D_SC])
    xin = _dot(hn, win_ref[:, 2 * D_SC:3 * D_SC])
    u = cg * xin
    prev = tail_scr[...]
    cw = cw_ref[...]
    a = _shift_rows(u, prev, 1) * cw[0:1] + u * cw[1:2]
    a_prev = pltpu.roll(prev, 1, axis=0) * cw[0:1] + prev * cw[1:2]
    conv = _shift_rows(a, a_prev, 1) + u * cw[2:3]
    tail_scr[...] = u[T - SUBLANES:T]
    sbuf_ref[0] = tail_scr[SUBLANES - (SC_CONV_K - 1):SUBLANES, :]

    bg = _dot(hn, win_ref[:, 0:D_SC])
    z = _dot(hn, win_ref[:, 3 * D_SC:4 * D_SC])
    yv = (bg * conv * _silu(z)).astype(BF16)
    out = _dot(yv, wout_ref[...]) + x
    y_ref[0] = _rms(out, fnw_ref[...])


def _prompt_sconv(x, nw, win, cw, wout, fnw):
    B, L, _ = x.shape
    T = SEQ_TILE
    kern = functools.partial(_prompt_sconv_kernel, seq_tile=T)
    consts = (nw, win, cw, wout, fnw)
    return pl.pallas_call(
        kern,
        grid=(B, L // T),
        in_specs=[pl.BlockSpec((1, T, D_MODEL), lambda b, j: (b, j, 0))] + [_const_spec(a.shape) for a in consts],
        out_specs=[
            pl.BlockSpec((1, T, D_MODEL), lambda b, j: (b, j, 0)),
            pl.BlockSpec((1, SC_CONV_K - 1, D_SC), lambda b, j: (b, 0, 0)),
        ],
        out_shape=[
            jax.ShapeDtypeStruct((B, L, D_MODEL), F32),
            jax.ShapeDtypeStruct((B, SC_CONV_K - 1, D_SC), F32),
        ],
        scratch_shapes=[pltpu.VMEM((SUBLANES, D_SC), F32)],
        compiler_params=pltpu.CompilerParams(
            dimension_semantics=("arbitrary", "arbitrary"), vmem_limit_bytes=VMEM_LIMIT),
        name="prompt_sconv_layer",
    )(x, *consts)


def _expand_heads(v):
    S = v.shape[0]
    low_half = lax.broadcasted_iota(jnp.int32, (S, LANES), 1) < HEAD_DIM
    parts = []
    for k in range(HEADS // 2):
        lo = jnp.broadcast_to(v[:, 2 * k:2 * k + 1], (S, LANES))
        hi = jnp.broadcast_to(v[:, 2 * k + 1:2 * k + 2], (S, LANES))
        parts.append(jnp.where(low_half, lo, hi))
    return jnp.concatenate(parts, axis=1)


def _sample_in_kernel(x_ref, nw_ref, wz_ref, wxbc_ref, wdt_ref, b0_ref, b1_ref, b2_ref, cw_ref, cb_ref,
                      dtb_ref, alog_ref, dskip_ref,
                      z_ref, xbc_ref, bm_ref, cm_ref, xdt_ref, decay_ref, decayx_ref, ypart_ref):
    hn = _rms(x_ref[...], nw_ref[...]).astype(BF16)
    z_ref[...] = _dot(hn, wz_ref[...])
    xbc = _dot(hn, wxbc_ref[...])
    xbc_ref[...] = xbc
    cw = cw_ref[...]
    conv = b0_ref[...] * cw[0:1]
    conv = conv + b1_ref[...] * cw[1:2]
    conv = conv + b2_ref[...] * cw[2:3]
    conv = conv + xbc * cw[3:4]
    xc = _silu(conv + cb_ref[...])
    xs = xc[:, :D_INNER]
    bm = xc[:, D_INNER:D_INNER + GROUPS * D_STATE]
    cm = xc[:, D_INNER + GROUPS * D_STATE:]
    bm_ref[...] = bm
    cm_ref[...] = cm

    dt = _softplus(_dot(hn, wdt_ref[...]) + dtb_ref[...])
    decay = jnp.exp(dt * -jnp.exp(alog_ref[...]))
    decay_ref[...] = decay
    decayx_ref[...] = _expand_heads(decay)
    xd = xs * _expand_heads(dt)
    xdt_ref[...] = xd.T

    cbx = []
    for g in range(GROUPS):
        sl = slice(g * D_STATE, (g + 1) * D_STATE)
        cbg = jnp.sum(cm[:, sl] * bm[:, sl], axis=-1, keepdims=True)
        cbx.append(jnp.broadcast_to(cbg, (xs.shape[0], GROUP_WIDTH)))
    ypart_ref[...] = jnp.concatenate(cbx, axis=1) * xd + dskip_ref[...] * xs


def _sample_in(x, nw, wz, wxbc, wdt, b0, b1, b2, cw, cb, dtb, alog, dskip):
    S = x.shape[0]
    shapes = [(S, D_INNER), (S, CONV_DIM), (S, GROUPS * D_STATE), (S, GROUPS * D_STATE), (D_INNER, S),
              (S, LANES), (S, D_INNER), (S, D_INNER)]
    return pl.pallas_call(
        _sample_in_kernel,
        out_shape=[jax.ShapeDtypeStruct(s, F32) for s in shapes],
        compiler_params=pltpu.CompilerParams(vmem_limit_bytes=VMEM_LIMIT),
        name="sample_ssd_in",
    )(x, nw, wz, wxbc, wdt, b0, b1, b2, cw, cb, dtb, alog, dskip)


def _sample_state_kernel(decay_ref, h0_ref, xdt_ref, bm_ref, cm_ref, hout_ref, yt_ref, *, block):
    S = xdt_ref.shape[1]

    @pl.when(pl.program_id(0) == 0)
    def _init():
        yt_ref[...] = jnp.zeros_like(yt_ref)

    lane = lax.broadcasted_iota(jnp.int32, (1, S), 1)
    for i in range(block):
        b = pl.program_id(0) * block + i
        sel = lane == b
        xd_col = jnp.sum(jnp.where(sel, xdt_ref[...], 0.0), axis=-1, keepdims=True)
        for g in range(GROUPS):
            rows = slice(g * GROUP_WIDTH, (g + 1) * GROUP_WIDTH)
            b_row = bm_ref[i, :, g * D_STATE:(g + 1) * D_STATE]
            c_row = cm_ref[i, :, g * D_STATE:(g + 1) * D_STATE]
            h0 = h0_ref[i, rows, :]
            y_col = jnp.sum(h0 * c_row, axis=-1, keepdims=True)
            yt_ref[rows, :] = jnp.where(sel, y_col, yt_ref[rows, :])
            upd = xd_col[rows] * b_row
            for e in range(HEADS_PER_GROUP):
                r = slice(g * GROUP_WIDTH + e * HEAD_DIM, g * GROUP_WIDTH + (e + 1) * HEAD_DIM)
                d = decay_ref[b, g * HEADS_PER_GROUP + e]
                hout_ref[i, r, :] = h0[e * HEAD_DIM:(e + 1) * HEAD_DIM] * d + upd[e * HEAD_DIM:(e + 1) * HEAD_DIM]


def _sample_state(decay, h0, xdt, bm, cm):
    S = h0.shape[0]
    blk = SAMPLE_BLOCK
    kern = functools.partial(_sample_state_kernel, block=blk)
    bm = bm.reshape(S, 1, GROUPS * D_STATE)
    cm = cm.reshape(S, 1, GROUPS * D_STATE)
    row_spec = pl.BlockSpec((blk, 1, GROUPS * D_STATE), lambda i: (i, 0, 0))
    return pl.pallas_call(
        kern,
        grid=(S // blk,),
        in_specs=[
            pl.BlockSpec(memory_space=pltpu.SMEM),
            pl.BlockSpec((blk, D_INNER, D_STATE), lambda i: (i, 0, 0)),
            pl.BlockSpec(xdt.shape, lambda i: (0, 0)),
            row_spec, row_spec,
        ],
        out_specs=[
            pl.BlockSpec((blk, D_INNER, D_STATE), lambda i: (i, 0, 0)),
            pl.BlockSpec((D_INNER, S), lambda i: (0, 0)),
        ],
        out_shape=[
            jax.ShapeDtypeStruct((S, D_INNER, D_STATE), F32),
            jax.ShapeDtypeStruct((D_INNER, S), F32),
        ],
        compiler_params=pltpu.CompilerParams(
            dimension_semantics=("arbitrary",), vmem_limit_bytes=VMEM_LIMIT),
        name="sample_ssd_state",
    )(decay, h0, xdt, bm, cm)


def _sample_out_kernel(x_ref, yt_ref, ypart_ref, decayx_ref, z_ref, gnw_ref, wout0_ref, nw1_ref, win_ref,
                       s0_ref, s1_ref, scw_ref, wout1_ref, fnw_ref,
                       y_ref, u_ref, x1_scr):
    y = ypart_ref[...] + decayx_ref[...] * yt_ref[...].T
    yn = _group_norm(y * _silu(z_ref[...]), gnw_ref[...]).astype(BF16)
    x1_scr[...] = x_ref[...] + _dot(yn, wout0_ref[...])

    hn = _rms(x1_scr[...], nw1_ref[...]).astype(BF16)
    cg = _dot(hn, win_ref[:, D_SC:2 * D_SC])
    xin = _dot(hn, win_ref[:, 2 * D_SC:3 * D_SC])
    u = cg * xin
    u_ref[...] = u
    cw = scw_ref[...]
    conv = s0_ref[...] * cw[0:1]
    conv = conv + s1_ref[...] * cw[1:2]
    conv = conv + u * cw[2:3]
    bg = _dot(hn, win_ref[:, 0:D_SC])
    z = _dot(hn, win_ref[:, 3 * D_SC:4 * D_SC])
    yv = (bg * conv * _silu(z)).astype(BF16)
    y_ref[...] = _rms(x1_scr[...] + _dot(yv, wout1_ref[...]), fnw_ref[...])


def _sample_out(x, yt, ypart, decayx, z, gnw, wout0, nw1, win, s0, s1, scw, wout1, fnw):
    S = x.shape[0]
    return pl.pallas_call(
        _sample_out_kernel,
        out_shape=[jax.ShapeDtypeStruct((S, D_MODEL), F32), jax.ShapeDtypeStruct((S, D_SC), F32)],
        scratch_shapes=[pltpu.VMEM((S, D_MODEL), F32)],
        compiler_params=pltpu.CompilerParams(vmem_limit_bytes=VMEM_LIMIT),
        name="sample_out",
    )(x, yt, ypart, decayx, z, gnw, wout0, nw1, win, s0, s1, scw, wout1, fnw)


def _pad_lanes(v):
    return jnp.pad(v, (0, LANES - v.shape[0])).reshape(1, LANES)


def kernel(x_prompt, x_sample, state_ssm, state_ssm_conv, state_sconv, norm_w, final_norm_w, ssd_w_in,
           ssd_conv_w, ssd_conv_b, ssd_dt_bias, ssd_A_log, ssd_D, ssd_norm_w, ssd_w_out, sc_w_in, sc_conv_w,
           sc_w_out):
    B = x_prompt.shape[0]
    S = x_sample.shape[0]

    w_in0 = ssd_w_in[0]
    wz = w_in0[:, :D_INNER].astype(BF16)
    wxbc = w_in0[:, D_INNER:D_INNER + CONV_DIM].astype(BF16)
    wdt = jnp.pad(w_in0[:, D_INNER + CONV_DIM:], ((0, 0), (0, LANES - HEADS))).astype(BF16)
    wout0 = ssd_w_out[0].astype(BF16)
    win1 = sc_w_in[0].astype(BF16)
    wout1 = sc_w_out[0].astype(BF16)
    nw0 = norm_w[0].reshape(1, D_MODEL)
    nw1 = norm_w[1].reshape(1, D_MODEL)
    fnw = final_norm_w.reshape(1, D_MODEL)
    cw0 = ssd_conv_w[0]
    cb0 = ssd_conv_b[0].reshape(1, CONV_DIM)
    dtb = _pad_lanes(ssd_dt_bias[0])
    alog = _pad_lanes(ssd_A_log[0])
    dskip = jnp.repeat(ssd_D[0], HEAD_DIM).reshape(1, D_INNER)
    gnw = ssd_norm_w[0].reshape(1, D_INNER)
    scw = sc_conv_w[0]

    y1, ssm_p, conv_p = _prompt_ssd(x_prompt, nw0, wz, wxbc, wdt, cw0, cb0, dtb, alog, dskip, gnw, wout0)
    y_prompt, sconv_p = _prompt_sconv(y1, nw1, win1, scw, wout1, fnw)

    xs2 = x_sample.reshape(S, D_MODEL)
    cbuf = state_ssm_conv[0]
    z, xbc_new, bm, cm, xdt, decay, decayx, ypart = _sample_in(
        xs2, nw0, wz, wxbc, wdt, cbuf[:, 0], cbuf[:, 1], cbuf[:, 2], cw0, cb0, dtb, alog, dskip)
    ssm_s, yt = _sample_state(decay[:, :HEADS], state_ssm[0].reshape(S, D_INNER, D_STATE), xdt, bm, cm)
    sbuf = state_sconv[0]
    y_sample, u_new = _sample_out(xs2, yt, ypart, decayx, z, gnw, wout0, nw1, win1, sbuf[:, 0], sbuf[:, 1],
                                  scw, wout1, fnw)

    return (
        y_prompt,
        y_sample.reshape(S, 1, D_MODEL),
        ssm_p.reshape(1, B, HEADS, HEAD_DIM, D_STATE),
        conv_p.reshape(1, B, SSD_CONV_K - 1, CONV_DIM),
        sconv_p.reshape(1, B, SC_CONV_K - 1, D_SC),
        ssm_s.reshape(1, S, HEADS, HEAD_DIM, D_STATE),
        jnp.stack([cbuf[:, 1], cbuf[:, 2], xbc_new], axis=1)[None],
        jnp.stack([sbuf[:, 1], u_new], axis=1)[None],
    )
```
